```python
import math
import jax, jax.numpy as jnp
from jax import lax
import numpy as np

D_MODEL = 1024
BATCH = 32
SEQ = 2048
DEPTH = 4

CHUNK = 64
PLE_DIM = 256
MIX_WIDTH = D_MODEL
POOL_WIDTH = MIX_WIDTH // 2
SSM_WIDTH = MIX_WIDTH - POOL_WIDTH
POOL_WINDOWS = (2, 4, 8, 16)
POOL_GROUP = POOL_WIDTH // len(POOL_WINDOWS)
SSM_GROUP_CH = 16
SSM_GROUPS = SSM_WIDTH // SSM_GROUP_CH
SSM_STATE = 64
DT_MIN = 1e-3
DT_MAX = 1e-1
A_RE_MAX = -1e-4
_FF_RAW = -(-(8 * D_MODEL) // 3)
D_FF = -(-_FF_RAW // 256) * 256
DEEPNORM_ALPHA = (2.0 * DEPTH) ** 0.25
DEEPNORM_BETA = (8.0 * DEPTH) ** -0.25
LN_EPS = 1e-5

kernel_name = "hybrid_pool_s5_deepnorm_encoder"


def layer_norm(x, g, b):
    xf = x.astype(jnp.float32)
    mu = jnp.mean(xf, axis=-1, keepdims=True)
    var = jnp.mean(jnp.square(xf - mu), axis=-1, keepdims=True)
    y = (xf - mu) * lax.rsqrt(var + LN_EPS) * g.astype(jnp.float32) + b.astype(jnp.float32)
    return y.astype(x.dtype)


def multiscale_pool(u, w, b, scale):
    bsz, s, _ = u.shape
    uf = u.astype(jnp.float32)
    pos = jnp.arange(1, s + 1, dtype=jnp.float32)[None, :, None]
    outs = []
    for g, win in enumerate(POOL_WINDOWS):
        ug = uf[..., g * POOL_GROUP:(g + 1) * POOL_GROUP]
        c = jnp.cumsum(ug, axis=1)
        lag = jnp.pad(c[:, :-win], ((0, 0), (win, 0), (0, 0)))
        mean = (c - lag) / jnp.minimum(pos, float(win))
        outs.append(mean - ug)
    z = jnp.stack(outs, axis=2)
    y = jnp.einsum('bsgc,gcd->bsgd', z, w.astype(jnp.float32)).reshape(bsz, s, POOL_WIDTH)
    return (y + b.astype(jnp.float32)) * scale.astype(jnp.float32)


def s5_mixer(u, a_re, a_im, log_dt, b_re, b_im, c_re, c_im, d, glu_w, glu_b):
    bsz, s, _ = u.shape
    f32 = jnp.float32
    uf = u.astype(f32)
    ug = uf.reshape(bsz, s, SSM_GROUPS, SSM_GROUP_CH)
    lam = lax.complex(jnp.minimum(a_re.astype(f32), A_RE_MAX), a_im.astype(f32))
    dt = jnp.exp(log_dt.astype(f32))[:, None]
    lam_bar = jnp.exp(lam * dt)
    b_mat = lax.complex(b_re.astype(f32), b_im.astype(f32))
    b_bar = ((lam_bar - 1.0) / lam)[..., None] * b_mat
    bu = lax.complex(jnp.einsum('bsgc,gnc->bsgn', ug, jnp.real(b_bar)),
                     jnp.einsum('bsgc,gnc->bsgn', ug, jnp.imag(b_bar)))
    a_seq = jnp.broadcast_to(lam_bar, (1, s) + lam_bar.shape)

    def combine(left, right):
        a_l, x_l = left
        a_r, x_r = right
        return a_r * a_l, a_r * x_l + x_r

    _, states = lax.associative_scan(combine, (a_seq, bu), axis=1)
    y = (jnp.einsum('bsgn,gcn->bsgc', jnp.real(states), c_re.astype(f32))
         - jnp.einsum('bsgn,gcn->bsgc', jnp.imag(states), c_im.astype(f32)))
    y = y.reshape(bsz, s, SSM_WIDTH) + d.astype(f32) * uf
    y = jax.nn.gelu(y)
    return y * jax.nn.sigmoid(y @ glu_w.astype(f32) + glu_b.astype(f32))


def setup_inputs(seed: int = 0) -> dict:
    key = jax.random.key(seed)
    ks = jax.random.split(key, 27)
    L, D, M, H = DEPTH, D_MODEL, MIX_WIDTH, D_FF
    G, N, C = SSM_GROUPS, SSM_STATE, SSM_GROUP_CH
    nrm = lambda k, shape, std: jax.random.normal(k, shape, jnp.float32) * std
    xavier = lambda fi, fo: math.sqrt(2.0 / (fi + fo))
    n_idx = jnp.arange(N, dtype=jnp.float32)
    return {
        "x": nrm(ks[0], (BATCH, SEQ, D), 1.0),
        "p": nrm(ks[1], (L, BATCH, SEQ, PLE_DIM), 1.0),
        "w_in": nrm(ks[2], (L, D, M), D ** -0.5),
        "pool_w": nrm(ks[3], (L, len(POOL_WINDOWS), POOL_GROUP, POOL_GROUP), POOL_GROUP ** -0.5),
        "pool_b": nrm(ks[4], (L, POOL_WIDTH), 0.01),
        "pool_scale": 1.0 + nrm(ks[5], (L, POOL_WIDTH), 0.02),
        "ssm_a_re": -0.5 + nrm(ks[6], (L, G, N), 0.01),
        "ssm_a_im": math.pi * n_idx[None, None, :] + nrm(ks[7], (L, G, N), 0.01),
        "ssm_log_dt": jax.random.uniform(ks[8], (L, G), jnp.float32, math.log(DT_MIN), math.log(DT_MAX)),
        "ssm_b_re": nrm(ks[9], (L, G, N, C), (2.0 * C) ** -0.5),
        "ssm_b_im": nrm(ks[10], (L, G, N, C), (2.0 * C) ** -0.5),
        "ssm_c_re": nrm(ks[11], (L, G, C, N), N ** -0.5),
        "ssm_c_im": nrm(ks[12], (L, G, C, N), N ** -0.5),
        "ssm_d": nrm(ks[13], (L, SSM_WIDTH), 1.0),
        "ssm_glu_w": nrm(ks[14], (L, SSM_WIDTH, SSM_WIDTH), SSM_WIDTH ** -0.5),
        "ssm_glu_b": nrm(ks[15], (L, SSM_WIDTH), 0.01),
        "w_out": nrm(ks[16], (L, M, D), xavier(M, D) * DEEPNORM_BETA),
        "ln1_g": 1.0 + nrm(ks[17], (L, D), 0.02),
        "ln1_b": nrm(ks[18], (L, D), 0.01),
        "ffn_w1": nrm(ks[19], (L, D, H), xavier(D, H)),
        "ffn_w3": nrm(ks[20], (L, D, H), xavier(D, H)),
        "ffn_w2": nrm(ks[21], (L, H, D), xavier(H, D) * DEEPNORM_BETA),
        "ple_w": nrm(ks[22], (L, PLE_DIM, D), xavier(PLE_DIM, D) * DEEPNORM_BETA),
        "ple_gate_w": nrm(ks[23], (L, D, D), D ** -0.5),
        "ln2_g": 1.0 + nrm(ks[24], (L, D), 0.02),
        "ln2_b": nrm(ks[25], (L, D), 0.01),
    }


def reference(x, p, w_in, pool_w, pool_b, pool_scale, ssm_a_re, ssm_a_im, ssm_log_dt,
              ssm_b_re, ssm_b_im, ssm_c_re, ssm_c_im, ssm_d, ssm_glu_w, ssm_glu_b,
              w_out, ln1_g, ln1_b, ffn_w1, ffn_w3, ffn_w2, ple_w, ple_gate_w, ln2_g, ln2_b):
    h = x
    for i in range(DEPTH):
        u = h @ w_in[i]
        y_pool = multiscale_pool(u[..., :POOL_WIDTH], pool_w[i], pool_b[i], pool_scale[i])
        y_ssm = s5_mixer(u[..., POOL_WIDTH:], ssm_a_re[i], ssm_a_im[i], ssm_log_dt[i],
                         ssm_b_re[i], ssm_b_im[i], ssm_c_re[i], ssm_c_im[i], ssm_d[i],
                         ssm_glu_w[i], ssm_glu_b[i])
        mix = jnp.concatenate([y_pool, y_ssm], axis=-1).astype(h.dtype) @ w_out[i]
        h = layer_norm(DEEPNORM_ALPHA * h + mix, ln1_g[i], ln1_b[i])
        f = (jax.nn.silu(h @ ffn_w1[i]) * (h @ ffn_w3[i])) @ ffn_w2[i]
        r = DEEPNORM_ALPHA * h + f
        e = (p[i] @ ple_w[i]) * jax.nn.sigmoid(r @ ple_gate_w[i])
        h = layer_norm(r + e, ln2_g[i], ln2_b[i])
    return h
```

```python
import functools
import math

import jax
import jax.numpy as jnp
from jax import lax
from jax.experimental import pallas as pl
from jax.experimental.pallas import tpu as pltpu

F32 = jnp.float32
BF16 = jnp.bfloat16

POOL_WINDOWS = (2, 4, 8, 16)
SSM_GROUP_CH = 16
A_RE_MAX = -1e-4
LN_EPS = 1e-5

LANES = 128
SUBLANES = 8
BF16_SUBLANES = 16
MXU_DIM = 256
VMEM_LIMIT_BYTES = 56 * 1024 * 1024

MIXER_TIME_TILE = 16
FFN_ROW_TILE = 512
FFN_HIDDEN_CHUNKS = 2
SCAN_UNROLL = 8


def _dot(a, b):
    return jnp.dot(a, b, preferred_element_type=F32)


def _layer_norm(x, g, b):
    mu = jnp.mean(x, axis=-1, keepdims=True)
    xc = x - mu
    var = jnp.mean(xc * xc, axis=-1, keepdims=True)
    return xc * lax.rsqrt(var + LN_EPS) * g + b


def _gelu_tanh(x):
    c = math.sqrt(2.0 / math.pi)
    return 0.5 * x * (1.0 + jnp.tanh(c * (x + 0.044715 * (x * x * x))))


def _mixer_kernel(h_ref, win_ref, poolw_ref, poolb_ref, pools_ref, wb_ref, lam_ref, wc_ref,
                  d_ref, gluw_ref, glub_ref, wout_ref, g_ref, b_ref, o_ref,
                  ubuf, invc, x_s, z_s, st_s, *, tt, nb, alpha):
    step = pl.program_id(0)
    rows = tt * nb
    pool_w = ubuf.shape[1]
    halo = ubuf.shape[0] - rows
    half = x_s.shape[1] // 2
    n_half = st_s.shape[1] // x_s.shape[1]
    ssm_w = d_ref.shape[1]
    lane_blocks_per_half = ssm_w // LANES // n_half

    @pl.when(step == 0)
    def _init():
        ubuf[0:halo, :] = jnp.zeros((halo, pool_w), F32)
        st_s[...] = jnp.zeros(st_s.shape, F32)

    @pl.when(step < 2)
    def _counts():
        t = step * tt + lax.broadcasted_iota(jnp.int32, (rows, pool_w), 0) // nb
        lane = lax.broadcasted_iota(jnp.int32, (rows, pool_w), 1)
        pool_group = pool_w // len(POOL_WINDOWS)
        win = jnp.full((rows, pool_w), POOL_WINDOWS[-1], jnp.int32)
        for gi in range(len(POOL_WINDOWS) - 2, -1, -1):
            win = jnp.where(lane < (gi + 1) * pool_group, POOL_WINDOWS[gi], win)
        invc[...] = 1.0 / jnp.minimum(t + 1, win).astype(F32)

    h = h_ref[...]
    u = _dot(h.astype(BF16), win_ref[...])

    ubuf[halo:, :] = u[:, :pool_w]
    pool_group = pool_w // len(POOL_WINDOWS)
    z_groups = []
    for gi, w in enumerate(POOL_WINDOWS):
        lo, hi = gi * pool_group, (gi + 1) * pool_group
        s = ubuf[halo - (w - 1) * nb:, lo:hi]
        span = 1
        while span < w:
            s = s[span * nb:, :] + s[:-span * nb, :]
            span *= 2
        z_groups.append(s * invc[:, lo:hi] - u[:, lo:hi])
    ubuf[0:halo, :] = ubuf[rows:rows + halo, :]
    z = jnp.concatenate(z_groups, axis=1).astype(BF16)
    y_pool = jnp.concatenate(
        [_dot(z[:, j * MXU_DIM:(j + 1) * MXU_DIM], poolw_ref[j]) for j in range(pool_w // MXU_DIM)],
        axis=1)
    y_pool = (y_pool + poolb_ref[...]) * pools_ref[...]

    u_ssm = u[:, pool_w:]
    ub = u_ssm.astype(BF16)
    n_sub = nb // SUBLANES
    y_halves = []
    for m in range(n_half):
        for k2 in range(lane_blocks_per_half):
            k = m * lane_blocks_per_half + k2
            lhs = ub[:, k * LANES:(k + 1) * LANES]
            wcols = wb_ref.shape[3]
            for ri in range(2):
                x_s[:, ri * half + k2 * wcols: ri * half + (k2 + 1) * wcols] = _dot(lhs, wb_ref[k, ri])
        for c in range(half // LANES):
            cre, cim = c * LANES, half + c * LANES
            sre, sim = m * 2 * half + cre, m * 2 * half + cim
            lr = lam_ref[0, :, m * half + cre: m * half + cre + LANES]
            li = lam_ref[1, :, m * half + cre: m * half + cre + LANES]
            zr0 = tuple(st_s[j * SUBLANES:(j + 1) * SUBLANES, sre:sre + LANES] for j in range(n_sub))
            zi0 = tuple(st_s[j * SUBLANES:(j + 1) * SUBLANES, sim:sim + LANES] for j in range(n_sub))

            def scan_step(t, carry, cre=cre, cim=cim, lr=lr, li=li):
                zr, zi = carry
                r0 = pl.multiple_of(t * nb, nb)
                nzr, nzi = [], []
                for j in range(n_sub):
                    rj = r0 + j * SUBLANES
                    xr = x_s[pl.ds(rj, SUBLANES), cre:cre + LANES]
                    xi = x_s[pl.ds(rj, SUBLANES), cim:cim + LANES]
                    nzr.append(lr * zr[j] - li * zi[j] + xr)
                    nzi.append(lr * zi[j] + li * zr[j] + xi)
                per = BF16_SUBLANES // SUBLANES
                for q in range(n_sub // per):
                    rq = r0 + q * BF16_SUBLANES
                    z_s[pl.ds(rq, BF16_SUBLANES), cre:cre + LANES] = jnp.concatenate(
                        nzr[q * per:(q + 1) * per], axis=0).astype(BF16)
                    z_s[pl.ds(rq, BF16_SUBLANES), cim:cim + LANES] = jnp.concatenate(
                        nzi[q * per:(q + 1) * per], axis=0).astype(BF16)
                return tuple(nzr), tuple(nzi)

            zr, zi = lax.fori_loop(0, tt, scan_step, (zr0, zi0), unroll=SCAN_UNROLL)
            for j in range(n_sub):
                st_s[j * SUBLANES:(j + 1) * SUBLANES, sre:sre + LANES] = zr[j]
                st_s[j * SUBLANES:(j + 1) * SUBLANES, sim:sim + LANES] = zi[j]
        y_halves.append(_dot(z_s[...], wc_ref[m]))
    y = jnp.concatenate(y_halves, axis=1) + d_ref[...] * u_ssm
    y = _gelu_tanh(y)
    y_ssm = y * jax.nn.sigmoid(_dot(y.astype(BF16), gluw_ref[...]) + glub_ref[...])

    mix = _dot(jnp.concatenate([y_pool, y_ssm], axis=1).astype(BF16), wout_ref[...])
    o_ref[...] = _layer_norm(alpha * h + mix, g_ref[...], b_ref[...])


def _ffn_kernel(h_ref, p_ref, w1_ref, w3_ref, w2_ref, wple_ref, wg_ref, g_ref, b_ref, o_ref, *, alpha):
    h = h_ref[...]
    hb = h.astype(BF16)
    d_ff = w1_ref.shape[1]
    hc = d_ff // FFN_HIDDEN_CHUNKS
    f = jnp.zeros(h.shape, F32)
    for c in range(FFN_HIDDEN_CHUNKS):
        a = _dot(hb, w1_ref[:, c * hc:(c + 1) * hc])
        b3 = _dot(hb, w3_ref[:, c * hc:(c + 1) * hc])
        gact = (a * jax.nn.sigmoid(a) * b3).astype(BF16)
        f = f + _dot(gact, w2_ref[c * hc:(c + 1) * hc, :])
    r = alpha * h + f
    gate = jax.nn.sigmoid(_dot(r.astype(BF16), wg_ref[...]))
    e = _dot(p_ref[...].astype(BF16), wple_ref[...]) * gate
    o_ref[...] = _layer_norm(r + e, g_ref[...], b_ref[...])


def _resident(shape):
    nd = len(shape)
    return pl.BlockSpec(shape, lambda i, _nd=nd: (0,) * _nd, pipeline_mode=pl.Buffered(1))


def _mixer_call(h, lw, *, nb, alpha):
    n_rows, d_model = h.shape
    tt = MIXER_TIME_TILE
    rows = tt * nb
    pool_w = lw["pool_b"].shape[1]
    ssm_w = lw["d"].shape[1]
    n_state2 = lw["wc"].shape[1]
    n_half = lw["wc"].shape[0]
    halo = max(POOL_WINDOWS) * nb
    assert tt >= max(POOL_WINDOWS) and rows >= halo and n_rows % rows == 0
    assert nb % BF16_SUBLANES == 0
    weights = (lw["w_in"], lw["pool_w"], lw["pool_b"], lw["pool_s"], lw["wb"], lw["lam"], lw["wc"],
               lw["d"], lw["glu_w"], lw["glu_b"], lw["w_out"], lw["ln1_g"], lw["ln1_b"])
    return pl.pallas_call(
        functools.partial(_mixer_kernel, tt=tt, nb=nb, alpha=alpha),
        grid=(n_rows // rows,),
        in_specs=[pl.BlockSpec((rows, d_model), lambda i: (i, 0))] + [_resident(w.shape) for w in weights],
        out_specs=pl.BlockSpec((rows, d_model), lambda i: (i, 0)),
        out_shape=jax.ShapeDtypeStruct((n_rows, d_model), F32),
        scratch_shapes=[
            pltpu.VMEM((halo + rows, pool_w), F32),
            pltpu.VMEM((rows, pool_w), F32),
            pltpu.VMEM((rows, n_state2), F32),
            pltpu.VMEM((rows, n_state2), BF16),
            pltpu.VMEM((nb, n_half * n_state2), F32),
        ],
        compiler_params=pltpu.CompilerParams(
            dimension_semantics=("arbitrary",), vmem_limit_bytes=VMEM_LIMIT_BYTES),
        name="mixer",
    )(h, *weights)


def _ffn_call(h, p, lw, *, alpha):
    n_rows, d_model = h.shape
    rows = min(FFN_ROW_TILE, n_rows)
    assert n_rows % rows == 0 and lw["w1"].shape[1] % (FFN_HIDDEN_CHUNKS * LANES) == 0
    weights = (lw["w1"], lw["w3"], lw["w2"], lw["ple_w"], lw["gate_w"], lw["ln2_g"], lw["ln2_b"])
    return pl.pallas_call(
        functools.partial(_ffn_kernel, alpha=alpha),
        grid=(n_rows // rows,),
        in_specs=[pl.BlockSpec((rows, d_model), lambda i: (i, 0)),
                  pl.BlockSpec((rows, p.shape[1]), lambda i: (i, 0))]
                 + [_resident(w.shape) for w in weights],
        out_specs=pl.BlockSpec((rows, d_model), lambda i: (i, 0)),
        out_shape=jax.ShapeDtypeStruct((n_rows, d_model), F32),
        compiler_params=pltpu.CompilerParams(
            dimension_semantics=("parallel",), vmem_limit_bytes=VMEM_LIMIT_BYTES),
        name="ffn",
    )(h, p, *weights)


def _block_diag(blocks):
    n, r, c = blocks.shape[-3:]
    eye = jnp.eye(n, dtype=blocks.dtype)
    out = blocks[..., :, :, None, :] * eye[:, None, :, None]
    return out.reshape(blocks.shape[:-3] + (n * r, n * c))


def _ssm_params(a_re, a_im, log_dt, b_re, b_im, c_re, c_im, nb):
    n_groups, n_state = a_re.shape
    ch = b_re.shape[2]
    lam_re = jnp.minimum(a_re, A_RE_MAX)
    lam_im = a_im
    dt = jnp.exp(log_dt)[:, None]
    mag = jnp.exp(lam_re * dt)
    lb_re = mag * jnp.cos(lam_im * dt)
    lb_im = mag * jnp.sin(lam_im * dt)
    den = lam_re * lam_re + lam_im * lam_im
    cf_re = ((lb_re - 1.0) * lam_re + lb_im * lam_im) / den
    cf_im = (lb_im * lam_re - (lb_re - 1.0) * lam_im) / den
    bb_re = cf_re[..., None] * b_re - cf_im[..., None] * b_im
    bb_im = cf_re[..., None] * b_im + cf_im[..., None] * b_re

    groups_per_block = LANES // ch
    n_blocks = n_groups // groups_per_block
    bb = jnp.stack([bb_re, bb_im], axis=0)
    bb = bb.reshape(2, n_blocks, groups_per_block, n_state, ch).transpose(1, 0, 2, 4, 3)
    wb = _block_diag(bb).astype(BF16)

    n_half = (n_groups * ch) // MXU_DIM
    gph = n_groups // n_half
    cc = jnp.stack([c_re, -c_im], axis=0)
    cc = cc.reshape(2, n_half, gph, ch, n_state).transpose(1, 0, 2, 4, 3)
    wc = _block_diag(cc)
    wc = wc.reshape(n_half, 2 * gph * n_state, gph * ch).astype(BF16)

    lam = jnp.stack([lb_re.reshape(-1), lb_im.reshape(-1)], axis=0)
    lam = jnp.broadcast_to(lam[:, None, :], (2, SUBLANES, n_groups * n_state))
    return wb, wc, lam


def kernel(x, p, w_in, pool_w, pool_b, pool_scale, ssm_a_re, ssm_a_im, ssm_log_dt, ssm_b_re, ssm_b_im,
           ssm_c_re, ssm_c_im, ssm_d, ssm_glu_w, ssm_glu_b, w_out, ln1_g, ln1_b, ffn_w1, ffn_w3, ffn_w2,
           ple_w, ple_gate_w, ln2_g, ln2_b):
    bsz, seq, d_model = x.shape
    depth = p.shape[0]
    alpha = (2.0 * depth) ** 0.25
    n_rows = bsz * seq
    row = lambda v: v.reshape(1, -1).astype(F32)

    h = x.transpose(1, 0, 2).reshape(n_rows, d_model)
    p_t = p.transpose(0, 2, 1, 3).reshape(depth, n_rows, p.shape[-1])

    for i in range(depth):
        wb, wc, lam = _ssm_params(ssm_a_re[i], ssm_a_im[i], ssm_log_dt[i], ssm_b_re[i], ssm_b_im[i],
                                  ssm_c_re[i], ssm_c_im[i], bsz)
        pw = pool_w[i]
        per = MXU_DIM // pw.shape[1]
        pool_bd = _block_diag(pw.reshape(pw.shape[0] // per, per, pw.shape[1], pw.shape[2])).astype(BF16)
        lw = dict(
            w_in=w_in[i].astype(BF16), pool_w=pool_bd, pool_b=row(pool_b[i]), pool_s=row(pool_scale[i]),
            wb=wb, lam=lam, wc=wc, d=row(ssm_d[i]), glu_w=ssm_glu_w[i].astype(BF16), glu_b=row(ssm_glu_b[i]),
            w_out=w_out[i].astype(BF16), ln1_g=row(ln1_g[i]), ln1_b=row(ln1_b[i]),
            w1=ffn_w1[i].astype(BF16), w3=ffn_w3[i].astype(BF16), w2=ffn_w2[i].astype(BF16),
            ple_w=ple_w[i].astype(BF16), gate_w=ple_gate_w[i].astype(BF16),
            ln2_g=row(ln2_g[i]), ln2_b=row(ln2_b[i]))
        h = _mixer_call(h, lw, nb=bsz, alpha=alpha)
        h = _ffn_call(h, p_t[i], lw, alpha=alpha)

    return h.reshape(seq, bsz, d_model).transpose(1, 0, 2)
```
